```python
import jax, jax.numpy as jnp
from jax import lax
import numpy as np

D_MODEL = 1024
BATCH = 4
SEQ = 4096
DEPTH = 1
DEC_BATCH = 128
DEC_SEQ = 1
PAST_LEN = 8192
PAGE_SIZE = 128

N_HEADS = 8
N_KV_HEADS = 2
HEAD_DIM = 64
ATTN_WIDTH = N_HEADS * HEAD_DIM
KV_WIDTH = N_KV_HEADS * HEAD_DIM
WINDOW = 128
BLOCK = 128
ROPE_THETA = 10000.0
ATTN_SCALE = HEAD_DIM ** -0.5
CONV_CH = D_MODEL - ATTN_WIDTH
CONV_K = 31
MIX_WIDTH = ATTN_WIDTH + CONV_CH
Q_END = ATTN_WIDTH
K_END = Q_END + KV_WIDTH
V_END = K_END + KV_WIDTH
IN_COLS = V_END + 2 * CONV_CH
D_FF = 2816
FFN_CONV_K = 3
PLE_DIM = 256
EPS = 1e-6
NEG_INF = -1e30

kernel_name = 'hymba_swa_sink_conformer_convffn_step'


def rms_norm(x, g):
    xf = x.astype(jnp.float32)
    y = xf * lax.rsqrt(jnp.mean(xf * xf, axis=-1, keepdims=True) + EPS)
    return (y * g.astype(jnp.float32)).astype(x.dtype)


def layer_norm(x, g, b):
    xf = x.astype(jnp.float32)
    xc = xf - jnp.mean(xf, axis=-1, keepdims=True)
    var = jnp.mean(xc * xc, axis=-1, keepdims=True)
    y = xc * lax.rsqrt(var + EPS) * g.astype(jnp.float32) + b.astype(jnp.float32)
    return y.astype(x.dtype)


def rope(x, pos):
    half = HEAD_DIM // 2
    inv_freq = ROPE_THETA ** (-jnp.arange(half, dtype=jnp.float32) / half)
    ang = pos.astype(jnp.float32)[:, None] * inv_freq[None, :]
    cos = jnp.cos(ang)[None, :, None, :]
    sin = jnp.sin(ang)[None, :, None, :]
    xf = x.astype(jnp.float32)
    x1, x2 = xf[..., :half], xf[..., half:]
    return jnp.concatenate([x1 * cos - x2 * sin, x2 * cos + x1 * sin], axis=-1).astype(x.dtype)


def causal_dwconv(x_ext, w, b):
    out = lax.conv_general_dilated(
        x_ext, w[:, None, :].astype(x_ext.dtype), window_strides=(1,), padding='VALID',
        dimension_numbers=('NWC', 'WIO', 'NWC'), feature_group_count=x_ext.shape[-1])
    return out + b.astype(out.dtype)


def sink_softmax_weights(s, mask, sink):
    s = jnp.where(mask, s, NEG_INF)
    m = jnp.maximum(jnp.max(s, axis=-1), sink)
    e = jnp.exp(s - m[..., None])
    denom = jnp.sum(e, axis=-1) + jnp.exp(sink - m)
    return e / denom[..., None]


def banded_swa(q, k, v, sinks):
    B, S = q.shape[0], q.shape[1]
    nb = S // BLOCK
    G = N_HEADS // N_KV_HEADS
    qb = q.reshape(B, nb, BLOCK, N_KV_HEADS, G, HEAD_DIM)
    kb = k.reshape(B, nb, BLOCK, N_KV_HEADS, HEAD_DIM)
    vb = v.reshape(B, nb, BLOCK, N_KV_HEADS, HEAD_DIM)
    shift = lambda t: jnp.concatenate([jnp.zeros_like(t[:, :1]), t[:, :-1]], axis=1)
    kk = jnp.concatenate([shift(kb), kb], axis=2)
    vv = jnp.concatenate([shift(vb), vb], axis=2)
    s = jnp.einsum('bnqkgd,bnjkd->bnkgqj', qb, kk,
                   preferred_element_type=jnp.float32) * ATTN_SCALE
    i = jnp.arange(BLOCK)[:, None]
    j = jnp.arange(2 * BLOCK)[None, :]
    d = i + BLOCK - j
    band = (d >= 0) & (d <= WINDOW)
    blk = jnp.arange(nb)[:, None, None]
    mask = band[None] & ((blk > 0) | (j[None] >= BLOCK))
    sink = sinks.astype(jnp.float32).reshape(N_KV_HEADS, G)[None, None, :, :, None]
    pr = sink_softmax_weights(s, mask[None, :, None, None], sink)
    o = jnp.einsum('bnkgqj,bnjkd->bnqkgd', pr, vv.astype(jnp.float32))
    return o.reshape(B, S, ATTN_WIDTH).astype(q.dtype)


def decode_swa(q, k, v, cache_k, cache_v, sinks):
    B, T = q.shape[0], q.shape[1]
    G = N_HEADS // N_KV_HEADS
    kk = jnp.concatenate([cache_k.astype(k.dtype), k], axis=1)
    vv = jnp.concatenate([cache_v.astype(v.dtype), v], axis=1)
    qh = q.reshape(B, T, N_KV_HEADS, G, HEAD_DIM)
    s = jnp.einsum('btkgd,bjkd->bkgtj', qh, kk,
                   preferred_element_type=jnp.float32) * ATTN_SCALE
    i = jnp.arange(T)[:, None]
    j = jnp.arange(WINDOW + T)[None, :]
    d = i + WINDOW - j
    mask = (d >= 0) & (d <= WINDOW)
    sink = sinks.astype(jnp.float32).reshape(N_KV_HEADS, G)[None, :, :, None]
    pr = sink_softmax_weights(s, mask, sink)
    o = jnp.einsum('bkgtj,bjkd->btkgd', pr, vv.astype(jnp.float32))
    return o.reshape(B, T, ATTN_WIDTH).astype(q.dtype), kk[:, -WINDOW:], vv[:, -WINDOW:]


def decoder_layer(x, p, pos, kv_cache, conv_hist, ffn_hist, lw):
    (norm_mix_g, w_in, q_norm_g, k_norm_g, attn_sinks, conv_w, conv_b, conv_ln_g,
     conv_ln_b, w_out, norm_ffn_g, w_ffn_in, ffn_conv_w, ffn_conv_b, w_ffn_out,
     norm_ple_g, w_ple_gate, w_ple_proj) = lw
    B, T = x.shape[0], x.shape[1]
    z = rms_norm(x, norm_mix_g) @ w_in
    q, k, v, cu = jnp.split(z, [Q_END, K_END, V_END], axis=-1)
    q = rope(rms_norm(q.reshape(B, T, N_HEADS, HEAD_DIM), q_norm_g), pos)
    k = rope(rms_norm(k.reshape(B, T, N_KV_HEADS, HEAD_DIM), k_norm_g), pos)
    v = v.reshape(B, T, N_KV_HEADS, HEAD_DIM)
    if kv_cache is None:
        attn_o = banded_swa(q, k, v, attn_sinks)
        new_k, new_v = k[:, -WINDOW:], v[:, -WINDOW:]
    else:
        attn_o, new_k, new_v = decode_swa(q, k, v, kv_cache[0], kv_cache[1], attn_sinks)
    a, g = jnp.split(cu, 2, axis=-1)
    glu = a * jax.nn.sigmoid(g)
    conv_ext = jnp.concatenate([conv_hist.astype(glu.dtype), glu], axis=1)
    conv_o = jax.nn.silu(layer_norm(causal_dwconv(conv_ext, conv_w, conv_b), conv_ln_g, conv_ln_b))
    h = x + jnp.concatenate([attn_o, conv_o], axis=-1) @ w_out
    fg, fu = jnp.split(rms_norm(h, norm_ffn_g) @ w_ffn_in, 2, axis=-1)
    ffn_ext = jnp.concatenate([ffn_hist.astype(fg.dtype), fg], axis=1)
    fc = causal_dwconv(ffn_ext, ffn_conv_w, ffn_conv_b)
    h = h + (jax.nn.gelu(fc, approximate=False) * fu) @ w_ffn_out
    gate = jax.nn.sigmoid(rms_norm(h, norm_ple_g) @ w_ple_gate)
    h = h + (p @ w_ple_proj) * gate
    return h, (new_k, new_v, conv_ext[:, -(CONV_K - 1):], ffn_ext[:, -(FFN_CONV_K - 1):])


def setup_inputs(seed: int = 0) -> dict:
    key = jax.random.key(seed)
    ks = jax.random.split(key, 32)
    f32 = jnp.float32
    nrm = lambda k, shape, scale: jax.random.normal(k, shape, f32) * scale
    gain = lambda k, n: 1.0 + 0.01 * jax.random.normal(k, (DEPTH, n), f32)
    return {
        'x_prompt': nrm(ks[0], (BATCH, SEQ, D_MODEL), 1.0),
        'x_sample': nrm(ks[1], (DEC_BATCH, DEC_SEQ, D_MODEL), 1.0),
        'state_attn_k': nrm(ks[2], (DEPTH, DEC_BATCH, WINDOW, N_KV_HEADS, HEAD_DIM), 1.0),
        'state_attn_v': nrm(ks[3], (DEPTH, DEC_BATCH, WINDOW, N_KV_HEADS, HEAD_DIM), 1.0),
        'state_conv': nrm(ks[4], (DEPTH, DEC_BATCH, CONV_K - 1, CONV_CH), 0.5),
        'state_ffn_conv': nrm(ks[5], (DEPTH, DEC_BATCH, FFN_CONV_K - 1, D_FF), 1.0),
        'p_prompt': nrm(ks[6], (DEPTH, BATCH, SEQ, PLE_DIM), 1.0),
        'p_sample': nrm(ks[7], (DEPTH, DEC_BATCH, DEC_SEQ, PLE_DIM), 1.0),
        'norm_mix_g': gain(ks[8], D_MODEL),
        'w_in': nrm(ks[9], (DEPTH, D_MODEL, IN_COLS), D_MODEL ** -0.5),
        'q_norm_g': gain(ks[10], HEAD_DIM),
        'k_norm_g': gain(ks[11], HEAD_DIM),
        'attn_sinks': nrm(ks[12], (DEPTH, N_HEADS), 0.5),
        'conv_w': nrm(ks[13], (DEPTH, CONV_K, CONV_CH), CONV_K ** -0.5),
        'conv_b': nrm(ks[14], (DEPTH, CONV_CH), 0.01),
        'conv_ln_g': gain(ks[15], CONV_CH),
        'conv_ln_b': nrm(ks[16], (DEPTH, CONV_CH), 0.01),
        'w_out': nrm(ks[17], (DEPTH, MIX_WIDTH, D_MODEL), MIX_WIDTH ** -0.5),
        'norm_ffn_g': gain(ks[18], D_MODEL),
        'w_ffn_in': nrm(ks[19], (DEPTH, D_MODEL, 2 * D_FF), D_MODEL ** -0.5),
        'ffn_conv_w': nrm(ks[20], (DEPTH, FFN_CONV_K, D_FF), FFN_CONV_K ** -0.5),
        'ffn_conv_b': nrm(ks[21], (DEPTH, D_FF), 0.01),
        'w_ffn_out': nrm(ks[22], (DEPTH, D_FF, D_MODEL), D_FF ** -0.5),
        'norm_ple_g': gain(ks[23], D_MODEL),
        'w_ple_gate': nrm(ks[24], (DEPTH, D_MODEL, D_MODEL), D_MODEL ** -0.5),
        'w_ple_proj': nrm(ks[25], (DEPTH, PLE_DIM, D_MODEL), PLE_DIM ** -0.5),
    }


def reference(x_prompt, x_sample, state_attn_k, state_attn_v, state_conv, state_ffn_conv,
              p_prompt, p_sample, norm_mix_g, w_in, q_norm_g, k_norm_g, attn_sinks,
              conv_w, conv_b, conv_ln_g, conv_ln_b, w_out, norm_ffn_g, w_ffn_in,
              ffn_conv_w, ffn_conv_b, w_ffn_out, norm_ple_g, w_ple_gate, w_ple_proj):
    bp, sp = x_prompt.shape[0], x_prompt.shape[1]
    ts = x_sample.shape[1]
    pos_prompt = jnp.arange(sp, dtype=jnp.int32)
    pos_sample = PAST_LEN + jnp.arange(ts, dtype=jnp.int32)
    conv_hist_p = jnp.zeros((bp, CONV_K - 1, CONV_CH), x_prompt.dtype)
    ffn_hist_p = jnp.zeros((bp, FFN_CONV_K - 1, D_FF), x_prompt.dtype)
    hp, hs = x_prompt, x_sample
    kp, vp, cp, fp = [], [], [], []
    ksl, vsl, csl, fsl = [], [], [], []
    for i in range(DEPTH):
        lw = (norm_mix_g[i], w_in[i], q_norm_g[i], k_norm_g[i], attn_sinks[i], conv_w[i],
              conv_b[i], conv_ln_g[i], conv_ln_b[i], w_out[i], norm_ffn_g[i], w_ffn_in[i],
              ffn_conv_w[i], ffn_conv_b[i], w_ffn_out[i], norm_ple_g[i], w_ple_gate[i],
              w_ple_proj[i])
        hp, st_p = decoder_layer(hp, p_prompt[i], pos_prompt, None, conv_hist_p, ffn_hist_p, lw)
        hs, st_s = decoder_layer(hs, p_sample[i], pos_sample,
                                 (state_attn_k[i], state_attn_v[i]),
                                 state_conv[i], state_ffn_conv[i], lw)
        kp.append(st_p[0]); vp.append(st_p[1]); cp.append(st_p[2]); fp.append(st_p[3])
        ksl.append(st_s[0]); vsl.append(st_s[1]); csl.append(st_s[2]); fsl.append(st_s[3])
    return (hp, hs,
            jnp.stack(kp), jnp.stack(vp), jnp.stack(cp), jnp.stack(fp),
            jnp.stack(ksl), jnp.stack(vsl), jnp.stack(csl), jnp.stack(fsl))
```

```python
import functools

import numpy as np
import jax
import jax.numpy as jnp
from jax import lax
from jax.experimental import pallas as pl
from jax.experimental.pallas import tpu as pltpu

D_MODEL = 1024
N_HEADS = 8
N_KV_HEADS = 2
HEAD_DIM = 64
GROUP = N_HEADS // N_KV_HEADS
ATTN_WIDTH = N_HEADS * HEAD_DIM
KV_WIDTH = N_KV_HEADS * HEAD_DIM
WINDOW = 128
BLOCK = 128
ROPE_THETA = 10000.0
ATTN_SCALE = HEAD_DIM ** -0.5
CONV_CH = D_MODEL - ATTN_WIDTH
CONV_K = 31
Q_END = ATTN_WIDTH
K_END = Q_END + KV_WIDTH
V_END = K_END + KV_WIDTH
IN_COLS = V_END + 2 * CONV_CH
D_FF = 2816
FFN_CONV_K = 3
PLE_DIM = 256
PAST_LEN = 8192
EPS = 1e-6
NEG_INF = -1e30

LANES = 128
SUBLANES = 8
VMEM_LIMIT_BYTES = 56 * 1024 * 1024

TM = 512
HIST_ROWS = 32
FFN_CHUNK = 256
SAMPLE_CHUNK = 16

BF16 = jnp.bfloat16
F32 = jnp.float32


def _dot(a, b):
    return jnp.dot(a, b, preferred_element_type=F32)


def _dot_nt(a, b):
    return lax.dot_general(a, b, (((1,), (1,)), ((), ())), preferred_element_type=F32)


def _rms_norm(x, g):
    return x * lax.rsqrt(jnp.mean(x * x, axis=-1, keepdims=True) + EPS) * g


def _sigmoid(x):
    return 1.0 / (1.0 + jnp.exp(-x))


def _gelu(x):
    return 0.5 * x * (1.0 + lax.erf(x * np.float32(np.sqrt(0.5))))


def _head_norm_rope(x, g, cos, sin):
    lane = lax.broadcasted_iota(jnp.int32, x.shape, 1)
    first_head = lane < HEAD_DIM
    sq = x * x
    ms_a = jnp.sum(jnp.where(first_head, sq, 0.0), axis=-1, keepdims=True)
    ms_b = jnp.sum(jnp.where(first_head, 0.0, sq), axis=-1, keepdims=True)
    ms = jnp.where(first_head, ms_a, ms_b) * (1.0 / HEAD_DIM)
    xn = x * lax.rsqrt(ms + EPS) * g
    first_half = (lane & (HEAD_DIM - 1)) < (HEAD_DIM // 2)
    partner = jnp.where(first_half,
                        pltpu.roll(xn, LANES - HEAD_DIM // 2, axis=1),
                        pltpu.roll(xn, HEAD_DIM // 2, axis=1))
    return xn * cos + partner * sin


def _layer_norm_silu(x, g, b):
    mu = jnp.mean(x, axis=-1, keepdims=True)
    xc = x - mu
    var = jnp.mean(xc * xc, axis=-1, keepdims=True)
    y = xc * lax.rsqrt(var + EPS) * g + b
    return y * _sigmoid(y)


def _kv_variants(x):
    lane = lax.broadcasted_iota(jnp.int32, x.shape, 1)
    lo = lane < HEAD_DIM
    sw = pltpu.roll(x, HEAD_DIM, axis=1)
    zero = jnp.zeros_like(x)
    return ((jnp.where(lo, x, zero).astype(BF16), jnp.where(lo, zero, sw).astype(BF16)),
            (jnp.where(lo, sw, zero).astype(BF16), jnp.where(lo, zero, x).astype(BF16)))


def _prompt_mixer_kernel(x_ref, cos_ref, sin_ref, gmix_ref, win_ref, qg_ref, kg_ref, sink_ref,
                         convw_ref, convb_ref, lng_ref, lnb_ref, wout_ref,
                         h_ref, newk_ref, newv_ref, newconv_ref,
                         kcarry_ref, vcarry_ref, ext_ref):
    t = pl.program_id(1)
    nt = pl.num_programs(1)

    @pl.when(t == 0)
    def _():
        kcarry_ref[...] = jnp.zeros_like(kcarry_ref)
        vcarry_ref[...] = jnp.zeros_like(vcarry_ref)
        ext_ref[0:HIST_ROWS, :] = jnp.zeros((HIST_ROWS, CONV_CH), F32)

    x = x_ref[0]
    xn = _rms_norm(x, gmix_ref[...]).astype(BF16)
    z = _dot(xn, win_ref[...])

    cos = cos_ref[...]
    sin = sin_ref[...]
    qg = qg_ref[...]
    q_blocks = []
    for c in range(ATTN_WIDTH // LANES):
        qc = _head_norm_rope(z[:, c * LANES:(c + 1) * LANES], qg, cos, sin)
        q_blocks.append((qc * ATTN_SCALE).astype(BF16))
    k_rot = _head_norm_rope(z[:, Q_END:K_END], kg_ref[...], cos, sin)
    v = z[:, K_END:V_END]

    k_ext = jnp.concatenate([kcarry_ref[...], k_rot], axis=0)
    v_ext = jnp.concatenate([vcarry_ref[...], v], axis=0)
    k_var = _kv_variants(k_ext)
    v_var = _kv_variants(v_ext)
    kcarry_ref[...] = k_rot[TM - BLOCK:, :]
    vcarry_ref[...] = v[TM - BLOCK:, :]

    @pl.when(t == nt - 1)
    def _():
        newk_ref[0] = k_rot[TM - WINDOW:, :]
        newv_ref[0] = v[TM - WINDOW:, :]

    rows = 2 * BLOCK
    ri = lax.broadcasted_iota(jnp.int32, (rows, 2 * BLOCK), 0) & (BLOCK - 1)
    cj = lax.broadcasted_iota(jnp.int32, (rows, 2 * BLOCK), 1)
    row_id = lax.broadcasted_iota(jnp.int32, (rows, 1), 0)
    prev_off_first = jnp.where(t > 0, 0, BLOCK)
    attn_rows = []
    for n in range(TM // BLOCK):
        prev_off = prev_off_first if n == 0 else 0
        mask = jnp.where(cj < BLOCK, cj - ri - prev_off, ri + BLOCK - cj) >= 0
        r0 = n * BLOCK
        out_cols = [None] * (ATTN_WIDTH // LANES)
        for kv in range(N_KV_HEADS):
            c0, c1 = 2 * kv, 2 * kv + 1
            qs = jnp.concatenate([q_blocks[c0][r0:r0 + BLOCK], q_blocks[c1][r0:r0 + BLOCK]], axis=0)
            for half in range(2):
                kmat = k_var[kv][half][r0:r0 + 2 * BLOCK]
                vmat = v_var[kv][half][r0:r0 + 2 * BLOCK]
                s = jnp.where(mask, _dot_nt(qs, kmat), NEG_INF)
                sink = jnp.where(row_id < BLOCK, sink_ref[2 * c0 + half], sink_ref[2 * c1 + half])
                m = jnp.maximum(jnp.max(s, axis=-1, keepdims=True), sink)
                e = jnp.exp(s - m)
                denom = jnp.sum(e, axis=-1, keepdims=True) + jnp.exp(sink - m)
                o = _dot(e.astype(BF16), vmat) / denom
                out_cols[c0] = o[:BLOCK] if half == 0 else out_cols[c0] + o[:BLOCK]
                out_cols[c1] = o[BLOCK:] if half == 0 else out_cols[c1] + o[BLOCK:]
        attn_rows.append(jnp.concatenate(out_cols, axis=1).astype(BF16))
    attn = jnp.concatenate(attn_rows, axis=0)

    a = z[:, V_END:V_END + CONV_CH]
    gate = z[:, V_END + CONV_CH:]
    glu = a * _sigmoid(gate)
    ext_ref[HIST_ROWS:, :] = glu
    acc = jnp.zeros((TM, CONV_CH), F32) + convb_ref[...]
    base = HIST_ROWS - (CONV_K - 1)
    for k in range(CONV_K):
        acc = acc + ext_ref[base + k:base + k + TM, :] * convw_ref[k:k + 1, :]
    conv_o = _layer_norm_silu(acc, lng_ref[...], lnb_ref[...]).astype(BF16)

    @pl.when(t == nt - 1)
    def _():
        newconv_ref[0] = ext_ref[HIST_ROWS + TM - (CONV_K - 1):, :]

    ext_ref[0:HIST_ROWS, :] = ext_ref[TM:TM + HIST_ROWS, :]

    h_ref[0] = x + _dot(attn, wout_ref[0:ATTN_WIDTH, :]) + _dot(conv_o, wout_ref[ATTN_WIDTH:, :])


def _ple_tail(h, p, gple, wgate_ref, wproj_ref):
    gate = _sigmoid(_dot(_rms_norm(h, gple).astype(BF16), wgate_ref[...]))
    return h + _dot(p.astype(BF16), wproj_ref[...]) * gate


def _prompt_ffn_kernel(h_ref, p_ref, gffn_ref, wfin_ref, fcw_ref, fcb_ref, wfout_ref,
                       gple_ref, wgate_ref, wproj_ref,
                       y_ref, newffn_ref,
                       fcarry_ref, buf_ref):
    t = pl.program_id(1)
    nt = pl.num_programs(1)

    @pl.when(t == 0)
    def _():
        fcarry_ref[...] = jnp.zeros_like(fcarry_ref)

    h = h_ref[0]
    hn = _rms_norm(h, gffn_ref[...]).astype(BF16)
    acc = h
    for c in range(D_FF // FFN_CHUNK):
        lo, hi = c * FFN_CHUNK, (c + 1) * FFN_CHUNK
        fg = _dot(hn, wfin_ref[:, lo:hi])
        fu = _dot(hn, wfin_ref[:, D_FF + lo:D_FF + hi])
        buf_ref[0:SUBLANES, :] = fcarry_ref[:, lo:hi]
        buf_ref[SUBLANES:, :] = fg
        fcarry_ref[:, lo:hi] = fg[TM - SUBLANES:, :]
        fc = (buf_ref[SUBLANES - 2:SUBLANES - 2 + TM, :] * fcw_ref[0:1, lo:hi]
              + buf_ref[SUBLANES - 1:SUBLANES - 1 + TM, :] * fcw_ref[1:2, lo:hi]
              + fg * fcw_ref[2:3, lo:hi] + fcb_ref[:, lo:hi])
        act = (_gelu(fc) * fu).astype(BF16)
        acc = acc + _dot(act, wfout_ref[lo:hi, :])

    @pl.when(t == nt - 1)
    def _():
        newffn_ref[0] = fcarry_ref[SUBLANES - (FFN_CONV_K - 1):, :]

    y_ref[0] = _ple_tail(acc, p_ref[0], gple_ref[...], wgate_ref, wproj_ref)


def _sample_proj_kernel(x_ref, cos_ref, sin_ref, gmix_ref, win_ref, qg_ref, kg_ref,
                        q_ref, k_ref, v_ref, glu_ref):
    xn = _rms_norm(x_ref[...], gmix_ref[...]).astype(BF16)
    z = _dot(xn, win_ref[...])
    cos = cos_ref[...]
    sin = sin_ref[...]
    for c in range(ATTN_WIDTH // LANES):
        qc = _head_norm_rope(z[:, c * LANES:(c + 1) * LANES], qg_ref[...], cos, sin)
        q_ref[:, c * LANES:(c + 1) * LANES] = qc * ATTN_SCALE
    k_ref[...] = _head_norm_rope(z[:, Q_END:K_END], kg_ref[...], cos, sin)
    v_ref[...] = z[:, K_END:V_END]
    glu_ref[...] = z[:, V_END:V_END + CONV_CH] * _sigmoid(z[:, V_END + CONV_CH:])


def _sample_mixer_kernel(q_ref, kn_ref, vn_ref, kc_ref, vc_ref, glu_ref, st_ref, sink_ref,
                         convw_ref, convb_ref, lng_ref, lnb_ref,
                         o_ref, co_ref):
    bc = SAMPLE_CHUNK
    q = q_ref[...]
    lane = lax.broadcasted_iota(jnp.int32, (bc, LANES), 1)
    lo = lane < HEAD_DIM
    pieces = []
    for hd in range(N_HEADS):
        c, kv, half = hd // 2, hd // GROUP, hd % 2
        blk = q[:, c * LANES:(c + 1) * LANES]
        if half != kv:
            blk = pltpu.roll(blk, HEAD_DIM, axis=1)
        keep = lo if kv == 0 else jnp.logical_not(lo)
        pieces.append(jnp.where(keep, blk, 0.0))
    q8 = jnp.concatenate(pieces, axis=0)
    rows = N_HEADS * bc
    kf = kc_ref[...].reshape(bc * WINDOW, KV_WIDTH)
    vf = vc_ref[...].reshape(bc * WINDOW, KV_WIDTH)
    s = _dot_nt(q8.astype(BF16), kf.astype(BF16))
    rb = lax.broadcasted_iota(jnp.int32, s.shape, 0) & (bc - 1)
    cb = lax.broadcasted_iota(jnp.int32, s.shape, 1) // WINDOW
    s = jnp.where(rb == cb, s, NEG_INF)
    kn8 = jnp.concatenate([kn_ref[...]] * N_HEADS, axis=0)
    vn8 = jnp.concatenate([vn_ref[...]] * N_HEADS, axis=0)
    s_new = jnp.sum(q8 * kn8, axis=-1, keepdims=True)
    sink = sink_ref[...]
    m = jnp.maximum(jnp.maximum(jnp.max(s, axis=-1, keepdims=True), s_new), sink)
    e = jnp.exp(s - m)
    e_new = jnp.exp(s_new - m)
    denom = jnp.sum(e, axis=-1, keepdims=True) + e_new + jnp.exp(sink - m)
    o = (_dot(e.astype(BF16), vf.astype(BF16)) + e_new * vn8) / denom
    o_ref[...] = o.reshape(N_HEADS, bc, KV_WIDTH)

    glu = glu_ref[...]
    acc = glu * convw_ref[CONV_K - 1:CONV_K, :] + convb_ref[...]
    for k in range(CONV_K - 1):
        acc = acc + st_ref[:, k, :] * convw_ref[k:k + 1, :]
    co_ref[...] = _layer_norm_silu(acc, lng_ref[...], lnb_ref[...])


def _sample_tail_kernel(x_ref, attn_ref, co_ref, w8_ref, woutc_ref, st0_ref, st1_ref, p_ref,
                        gffn_ref, wfin_ref, fcw_ref, fcb_ref, wfout_ref,
                        gple_ref, wgate_ref, wproj_ref,
                        y_ref, fg_ref):
    h = x_ref[...] + _dot(co_ref[...].astype(BF16), woutc_ref[...])
    for hd in range(N_HEADS):
        h = h + _dot(attn_ref[hd].astype(BF16), w8_ref[hd])
    y_ref[...] = h
    h = y_ref[...]
    hn = _rms_norm(h, gffn_ref[...]).astype(BF16)
    acc = h
    for c in range(D_FF // FFN_CHUNK):
        lo, hi = c * FFN_CHUNK, (c + 1) * FFN_CHUNK
        fg = _dot(hn, wfin_ref[:, lo:hi])
        fu = _dot(hn, wfin_ref[:, D_FF + lo:D_FF + hi])
        fg_ref[:, lo:hi] = fg
        fc = (st0_ref[:, lo:hi] * fcw_ref[0:1, lo:hi] + st1_ref[:, lo:hi] * fcw_ref[1:2, lo:hi]
              + fg * fcw_ref[2:3, lo:hi] + fcb_ref[:, lo:hi])
        act = (_gelu(fc) * fu).astype(BF16)
        acc = acc + _dot(act, wfout_ref[lo:hi, :])
    y_ref[...] = _ple_tail(acc, p_ref[...], gple_ref[...], wgate_ref, wproj_ref)


def _rope_tables(pos):
    half = HEAD_DIM // 2
    inv_freq = ROPE_THETA ** (-jnp.arange(half, dtype=F32) / half)
    ang = pos.astype(F32)[:, None] * inv_freq[None, :]
    cos = jnp.cos(ang)
    sin = jnp.sin(ang)
    cos_t = jnp.tile(cos, (1, LANES // half))
    sin_t = jnp.tile(jnp.concatenate([-sin, sin], axis=1), (1, LANES // HEAD_DIM))
    return cos_t, sin_t


def _const_spec(shape):
    zeros = (0,) * len(shape)
    return pl.BlockSpec(shape, lambda *_: zeros, pipeline_mode=pl.Buffered(1))


def _smem_spec():
    return pl.BlockSpec(memory_space=pltpu.SMEM)


def kernel(x_prompt, x_sample, state_attn_k, state_attn_v, state_conv, state_ffn_conv, p_prompt, p_sample, norm_mix_g, w_in, q_norm_g, k_norm_g, attn_sinks, conv_w, conv_b, conv_ln_g, conv_ln_b, w_out, norm_ffn_g, w_ffn_in, ffn_conv_w, ffn_conv_b, w_ffn_out, norm_ple_g, w_ple_gate, w_ple_proj):
    bp, sp, _ = x_prompt.shape
    bs = x_sample.shape[0]
    assert sp % TM == 0 and TM % BLOCK == 0 and bs % SAMPLE_CHUNK == 0
    assert norm_mix_g.shape[0] == 1 and x_sample.shape[1] == 1
    nt = sp // TM

    win = w_in[0].astype(BF16)
    wout = w_out[0].astype(BF16)
    wfin = w_ffn_in[0].astype(BF16)
    wfout = w_ffn_out[0].astype(BF16)
    wgate = w_ple_gate[0].astype(BF16)
    wproj = w_ple_proj[0].astype(BF16)
    gmix = norm_mix_g[0][None, :]
    gffn = norm_ffn_g[0][None, :]
    gple = norm_ple_g[0][None, :]
    qg = jnp.tile(q_norm_g[0], LANES // HEAD_DIM)[None, :]
    kg = jnp.tile(k_norm_g[0], LANES // HEAD_DIM)[None, :]
    sinks = attn_sinks[0]
    convw = conv_w[0]
    convb = conv_b[0][None, :]
    lng = conv_ln_g[0][None, :]
    lnb = conv_ln_b[0][None, :]
    fcw = ffn_conv_w[0]
    fcb = ffn_conv_b[0][None, :]

    cos_p, sin_p = _rope_tables(jnp.arange(sp, dtype=jnp.int32))
    cos_s, sin_s = _rope_tables(PAST_LEN + jnp.arange(1, dtype=jnp.int32))

    tile_spec = lambda w: pl.BlockSpec((1, TM, w), lambda b, t: (b, t, 0))
    state_spec = lambda r, w: pl.BlockSpec((1, r, w), lambda b, t: (b, 0, 0))
    params = pltpu.CompilerParams(dimension_semantics=("arbitrary", "arbitrary"),
                                  vmem_limit_bytes=VMEM_LIMIT_BYTES)
    h_p, newk_p, newv_p, newconv_p = pl.pallas_call(
        _prompt_mixer_kernel,
        grid=(bp, nt),
        in_specs=[tile_spec(D_MODEL),
                  pl.BlockSpec((TM, LANES), lambda b, t: (t, 0)),
                  pl.BlockSpec((TM, LANES), lambda b, t: (t, 0)),
                  _const_spec((1, D_MODEL)), _const_spec((D_MODEL, IN_COLS)),
                  _const_spec((1, LANES)), _const_spec((1, LANES)), _smem_spec(),
                  _const_spec((CONV_K, CONV_CH)), _const_spec((1, CONV_CH)),
                  _const_spec((1, CONV_CH)), _const_spec((1, CONV_CH)),
                  _const_spec((D_MODEL, D_MODEL))],
        out_specs=[tile_spec(D_MODEL), state_spec(WINDOW, KV_WIDTH), state_spec(WINDOW, KV_WIDTH),
                   state_spec(CONV_K - 1, CONV_CH)],
        out_shape=[jax.ShapeDtypeStruct((bp, sp, D_MODEL), F32),
                   jax.ShapeDtypeStruct((bp, WINDOW, KV_WIDTH), F32),
                   jax.ShapeDtypeStruct((bp, WINDOW, KV_WIDTH), F32),
                   jax.ShapeDtypeStruct((bp, CONV_K - 1, CONV_CH), F32)],
        scratch_shapes=[pltpu.VMEM((BLOCK, KV_WIDTH), F32), pltpu.VMEM((BLOCK, KV_WIDTH), F32),
                        pltpu.VMEM((HIST_ROWS + TM, CONV_CH), F32)],
        compiler_params=params,
        name="prompt_mixer",
    )(x_prompt, cos_p, sin_p, gmix, win, qg, kg, sinks, convw, convb, lng, lnb, wout)

    y_p, newffn_p = pl.pallas_call(
        _prompt_ffn_kernel,
        grid=(bp, nt),
        in_specs=[tile_spec(D_MODEL), tile_spec(PLE_DIM),
                  _const_spec((1, D_MODEL)), _const_spec((D_MODEL, 2 * D_FF)),
                  _const_spec((FFN_CONV_K, D_FF)), _const_spec((1, D_FF)),
                  _const_spec((D_FF, D_MODEL)),
                  _const_spec((1, D_MODEL)), _const_spec((D_MODEL, D_MODEL)),
                  _const_spec((PLE_DIM, D_MODEL))],
        out_specs=[tile_spec(D_MODEL), state_spec(FFN_CONV_K - 1, D_FF)],
        out_shape=[jax.ShapeDtypeStruct((bp, sp, D_MODEL), F32),
                   jax.ShapeDtypeStruct((bp, FFN_CONV_K - 1, D_FF), F32)],
        scratch_shapes=[pltpu.VMEM((SUBLANES, D_FF), F32), pltpu.VMEM((SUBLANES + TM, FFN_CHUNK), F32)],
        compiler_params=params,
        name="prompt_ffn",
    )(h_p, p_prompt[0], gffn, wfin, fcw, fcb, wfout, gple, wgate, wproj)

    xs = x_sample[:, 0, :]
    q_s, k_s, v_s, glu_s = pl.pallas_call(
        _sample_proj_kernel,
        out_shape=[jax.ShapeDtypeStruct((bs, ATTN_WIDTH), F32),
                   jax.ShapeDtypeStruct((bs, KV_WIDTH), F32),
                   jax.ShapeDtypeStruct((bs, KV_WIDTH), F32),
                   jax.ShapeDtypeStruct((bs, CONV_CH), F32)],
        compiler_params=pltpu.CompilerParams(vmem_limit_bytes=VMEM_LIMIT_BYTES),
        name="sample_proj",
    )(xs, cos_s, sin_s, gmix, win, qg, kg)

    kc = state_attn_k[0].reshape(bs, WINDOW, KV_WIDTH)
    vc = state_attn_v[0].reshape(bs, WINDOW, KV_WIDTH)
    stc = state_conv[0]
    sink_col = jnp.repeat(sinks, SAMPLE_CHUNK)[:, None]
    bc = SAMPLE_CHUNK
    row_spec = lambda w: pl.BlockSpec((bc, w), lambda i: (i, 0))
    cache_spec = pl.BlockSpec((bc, WINDOW, KV_WIDTH), lambda i: (i, 0, 0))
    attn_s, convo_s = pl.pallas_call(
        _sample_mixer_kernel,
        grid=(bs // bc,),
        in_specs=[row_spec(ATTN_WIDTH), row_spec(KV_WIDTH), row_spec(KV_WIDTH), cache_spec, cache_spec,
                  row_spec(CONV_CH), pl.BlockSpec((bc, CONV_K - 1, CONV_CH), lambda i: (i, 0, 0)),
                  pl.BlockSpec((N_HEADS * bc, 1), lambda i: (0, 0)),
                  pl.BlockSpec((CONV_K, CONV_CH), lambda i: (0, 0)),
                  pl.BlockSpec((1, CONV_CH), lambda i: (0, 0)),
                  pl.BlockSpec((1, CONV_CH), lambda i: (0, 0)),
                  pl.BlockSpec((1, CONV_CH), lambda i: (0, 0))],
        out_specs=[pl.BlockSpec((N_HEADS, bc, KV_WIDTH), lambda i: (0, i, 0)), row_spec(CONV_CH)],
        out_shape=[jax.ShapeDtypeStruct((N_HEADS, bs, KV_WIDTH), F32),
                   jax.ShapeDtypeStruct((bs, CONV_CH), F32)],
        compiler_params=pltpu.CompilerParams(dimension_semantics=("arbitrary",),
                                             vmem_limit_bytes=VMEM_LIMIT_BYTES),
        name="sample_mixer",
    )(q_s, k_s, v_s, kc, vc, glu_s, stc, sink_col, convw, convb, lng, lnb)

    wo_heads = wout[:ATTN_WIDTH].reshape(N_HEADS, HEAD_DIM, D_MODEL)
    zeros_h = jnp.zeros_like(wo_heads)
    head_kv = (jnp.arange(N_HEADS) // GROUP)[:, None, None]
    w8 = jnp.concatenate([jnp.where(head_kv == 0, wo_heads, zeros_h),
                          jnp.where(head_kv == 1, wo_heads, zeros_h)], axis=1)
    y_s, fg_s = pl.pallas_call(
        _sample_tail_kernel,
        out_shape=[jax.ShapeDtypeStruct((bs, D_MODEL), F32), jax.ShapeDtypeStruct((bs, D_FF), F32)],
        compiler_params=pltpu.CompilerParams(vmem_limit_bytes=VMEM_LIMIT_BYTES),
        name="sample_tail",
    )(xs, attn_s, convo_s, w8, wout[ATTN_WIDTH:], state_ffn_conv[0, :, 0, :], state_ffn_conv[0, :, 1, :],
      p_sample[0, :, 0, :], gffn, wfin, fcw, fcb, wfout, gple, wgate, wproj)

    kv_shape = (1, bp, WINDOW, N_KV_HEADS, HEAD_DIM)
    new_k_s = jnp.concatenate([state_attn_k[0][:, 1:], k_s.reshape(bs, 1, N_KV_HEADS, HEAD_DIM)], axis=1)
    new_v_s = jnp.concatenate([state_attn_v[0][:, 1:], v_s.reshape(bs, 1, N_KV_HEADS, HEAD_DIM)], axis=1)
    new_conv_s = jnp.concatenate([state_conv[0][:, 1:], glu_s[:, None, :]], axis=1)
    new_ffn_s = jnp.stack([state_ffn_conv[0, :, 1, :], fg_s], axis=1)
    return (y_p, y_s[:, None, :],
            newk_p.reshape(kv_shape), newv_p.reshape(kv_shape), newconv_p[None], newffn_p[None],
            new_k_s[None], new_v_s[None], new_conv_s[None], new_ffn_s[None])
```

```python
import functools

import numpy as np
import jax
import jax.numpy as jnp
from jax import lax
from jax.experimental import pallas as pl
from jax.experimental.pallas import tpu as pltpu

D_MODEL = 1024
N_HEADS = 8
N_KV_HEADS = 2
HEAD_DIM = 64
GROUP = N_HEADS // N_KV_HEADS
ATTN_WIDTH = N_HEADS * HEAD_DIM
KV_WIDTH = N_KV_HEADS * HEAD_DIM
WINDOW = 128
BLOCK = 128
ROPE_THETA = 10000.0
ATTN_SCALE = HEAD_DIM ** -0.5
CONV_CH = D_MODEL - ATTN_WIDTH
CONV_K = 31
Q_END = ATTN_WIDTH
K_END = Q_END + KV_WIDTH
V_END = K_END + KV_WIDTH
IN_COLS = V_END + 2 * CONV_CH
D_FF = 2816
FFN_CONV_K = 3
PLE_DIM = 256
PAST_LEN = 8192
EPS = 1e-6
NEG_INF = -1e30

LANES = 128
SUBLANES = 8
VMEM_LIMIT_BYTES = 56 * 1024 * 1024

TM = 512
HIST_ROWS = 32
FFN_CHUNK = 256
CONV_ROWS = 32
SAMPLE_CHUNK = 16

BF16 = jnp.bfloat16
F32 = jnp.float32


def _dot(a, b):
    return jnp.dot(a, b, preferred_element_type=F32)


def _dot_nt(a, b):
    return lax.dot_general(a, b, (((1,), (1,)), ((), ())), preferred_element_type=F32)


def _rms_norm(x, g):
    return x * lax.rsqrt(jnp.mean(x * x, axis=-1, keepdims=True) + EPS) * g


def _sigmoid(x):
    return 1.0 / (1.0 + jnp.exp(-x))


def _gelu(x):
    return 0.5 * x * (1.0 + lax.erf(x * np.float32(np.sqrt(0.5))))


def _head_norm_rope(x, g, cos, sin):
    lane = lax.broadcasted_iota(jnp.int32, x.shape, 1)
    first_head = lane < HEAD_DIM
    sq = x * x
    ms_a = jnp.sum(jnp.where(first_head, sq, 0.0), axis=-1, keepdims=True)
    ms_b = jnp.sum(jnp.where(first_head, 0.0, sq), axis=-1, keepdims=True)
    ms = jnp.where(first_head, ms_a, ms_b) * (1.0 / HEAD_DIM)
    xn = x * lax.rsqrt(ms + EPS) * g
    first_half = (lane & (HEAD_DIM - 1)) < (HEAD_DIM // 2)
    partner = jnp.where(first_half,
                        pltpu.roll(xn, LANES - HEAD_DIM // 2, axis=1),
                        pltpu.roll(xn, HEAD_DIM // 2, axis=1))
    return xn * cos + partner * sin


def _shift_rows(x, carry, s):
    rolled = pltpu.roll(x, s, axis=0)
    row = lax.broadcasted_iota(jnp.int32, carry.shape, 0)
    head = jnp.where(row < s, pltpu.roll(carry, s, axis=0), rolled[0:SUBLANES])
    return jnp.concatenate([head, rolled[SUBLANES:]], axis=0)


def _layer_norm_silu(x, g, b):
    mu = jnp.mean(x, axis=-1, keepdims=True)
    xc = x - mu
    var = jnp.mean(xc * xc, axis=-1, keepdims=True)
    y = xc * lax.rsqrt(var + EPS) * g + b
    return y * _sigmoid(y)


def _kv_variants(x):
    lane = lax.broadcasted_iota(jnp.int32, x.shape, 1)
    lo = lane < HEAD_DIM
    sw = pltpu.roll(x, HEAD_DIM, axis=1)
    zero = jnp.zeros_like(x)
    return ((jnp.where(lo, x, zero).astype(BF16), jnp.where(lo, zero, sw).astype(BF16)),
            (jnp.where(lo, sw, zero).astype(BF16), jnp.where(lo, zero, x).astype(BF16)))


def _prompt_mixer_kernel(x_ref, cos_ref, sin_ref, gmix_ref, win_ref, qg_ref, kg_ref, sink_ref,
                         convw_ref, convb_ref, lng_ref, lnb_ref, wout_ref,
                         h_ref, newk_ref, newv_ref, newconv_ref,
                         kcarry_ref, vcarry_ref, ext_ref, sh_ref):
    t = pl.program_id(1)
    nt = pl.num_programs(1)

    @pl.when(t == 0)
    def _():
        kcarry_ref[...] = jnp.zeros_like(kcarry_ref)
        vcarry_ref[...] = jnp.zeros_like(vcarry_ref)
        ext_ref[0:HIST_ROWS, :] = jnp.zeros((HIST_ROWS, CONV_CH), F32)

    x = x_ref[0]
    xn = _rms_norm(x, gmix_ref[...]).astype(BF16)
    z = _dot(xn, win_ref[...])

    cos = cos_ref[...]
    sin = sin_ref[...]
    qg = qg_ref[...]
    q_blocks = []
    for c in range(ATTN_WIDTH // LANES):
        qc = _head_norm_rope(z[:, c * LANES:(c + 1) * LANES], qg, cos, sin)
        q_blocks.append((qc * ATTN_SCALE).astype(BF16))
    k_rot = _head_norm_rope(z[:, Q_END:K_END], kg_ref[...], cos, sin)
    v = z[:, K_END:V_END]

    k_ext = jnp.concatenate([kcarry_ref[...], k_rot], axis=0)
    v_ext = jnp.concatenate([vcarry_ref[...], v], axis=0)
    k_var = _kv_variants(k_ext)
    v_var = _kv_variants(v_ext)
    kcarry_ref[...] = k_rot[TM - BLOCK:, :]
    vcarry_ref[...] = v[TM - BLOCK:, :]

    @pl.when(t == nt - 1)
    def _():
        newk_ref[0] = k_rot[TM - WINDOW:, :]
        newv_ref[0] = v[TM - WINDOW:, :]

    rows = 2 * BLOCK
    ri = lax.broadcasted_iota(jnp.int32, (rows, 2 * BLOCK), 0) & (BLOCK - 1)
    cj = lax.broadcasted_iota(jnp.int32, (rows, 2 * BLOCK), 1)
    row_id = lax.broadcasted_iota(jnp.int32, (rows, 1), 0)
    prev_off_first = jnp.where(t > 0, 0, BLOCK)
    attn_rows = []
    for n in range(TM // BLOCK):
        prev_off = prev_off_first if n == 0 else 0
        mask = jnp.where(cj < BLOCK, cj - ri - prev_off, ri + BLOCK - cj) >= 0
        r0 = n * BLOCK
        out_cols = [None] * (ATTN_WIDTH // LANES)
        for kv in range(N_KV_HEADS):
            c0, c1 = 2 * kv, 2 * kv + 1
            qs = jnp.concatenate([q_blocks[c0][r0:r0 + BLOCK], q_blocks[c1][r0:r0 + BLOCK]], axis=0)
            for half in range(2):
                kmat = k_var[kv][half][r0:r0 + 2 * BLOCK]
                vmat = v_var[kv][half][r0:r0 + 2 * BLOCK]
                s = jnp.where(mask, _dot_nt(qs, kmat), NEG_INF)
                sink = jnp.where(row_id < BLOCK, sink_ref[2 * c0 + half], sink_ref[2 * c1 + half])
                m = jnp.maximum(jnp.max(s, axis=-1, keepdims=True), sink)
                e = jnp.exp(s - m)
                denom = jnp.sum(e, axis=-1, keepdims=True) + jnp.exp(sink - m)
                o = _dot(e.astype(BF16), vmat) / denom
                out_cols[c0] = o[:BLOCK] if half == 0 else out_cols[c0] + o[:BLOCK]
                out_cols[c1] = o[BLOCK:] if half == 0 else out_cols[c1] + o[BLOCK:]
        attn_rows.append(jnp.concatenate(out_cols, axis=1).astype(BF16))
    attn = jnp.concatenate(attn_rows, axis=0)

    a = z[:, V_END:V_END + CONV_CH]
    gate = z[:, V_END + CONV_CH:]
    glu = a * _sigmoid(gate)
    ext_ref[HIST_ROWS:, :] = glu
    span = TM + HIST_ROWS - SUBLANES
    for r in range(1, SUBLANES):
        sh_ref[r - 1, 0:span, :] = ext_ref[r:r + span, :]
    base = HIST_ROWS - (CONV_K - 1)
    acc_rows = []
    for r0 in range(0, TM, CONV_ROWS):
        acc = jnp.zeros((CONV_ROWS, CONV_CH), F32) + convb_ref[...]
        for k in range(CONV_K):
            off = base + k
            a8, r = off - off % SUBLANES, off % SUBLANES
            if r == 0:
                tap = ext_ref[r0 + a8:r0 + a8 + CONV_ROWS, :]
            else:
                tap = sh_ref[r - 1, r0 + a8:r0 + a8 + CONV_ROWS, :]
            acc = acc + tap * convw_ref[k:k + 1, :]
        acc_rows.append(acc)
    acc = jnp.concatenate(acc_rows, axis=0)
    conv_o = _layer_norm_silu(acc, lng_ref[...], lnb_ref[...]).astype(BF16)

    @pl.when(t == nt - 1)
    def _():
        newconv_ref[0] = ext_ref[HIST_ROWS + TM - (CONV_K - 1):, :]

    ext_ref[0:HIST_ROWS, :] = ext_ref[TM:TM + HIST_ROWS, :]

    h_ref[0] = x + _dot(attn, wout_ref[0:ATTN_WIDTH, :]) + _dot(conv_o, wout_ref[ATTN_WIDTH:, :])


def _ple_tail(h, p, gple, wgate_ref, wproj_ref):
    gate = _sigmoid(_dot(_rms_norm(h, gple).astype(BF16), wgate_ref[...]))
    return h + _dot(p.astype(BF16), wproj_ref[...]) * gate


def _prompt_ffn_kernel(h_ref, p_ref, gffn_ref, wfin_ref, fcw_ref, fcb_ref, wfout_ref,
                       gple_ref, wgate_ref, wproj_ref,
                       y_ref, newffn_ref,
                       fcarry_ref):
    t = pl.program_id(1)
    nt = pl.num_programs(1)

    @pl.when(t == 0)
    def _():
        fcarry_ref[...] = jnp.zeros_like(fcarry_ref)

    h = h_ref[0]
    hn = _rms_norm(h, gffn_ref[...]).astype(BF16)
    acc = h
    n_chunks = D_FF // FFN_CHUNK
    up = lambda c: (_dot(hn, wfin_ref[:, c * FFN_CHUNK:(c + 1) * FFN_CHUNK]),
                    _dot(hn, wfin_ref[:, D_FF + c * FFN_CHUNK:D_FF + (c + 1) * FFN_CHUNK]))
    nxt = up(0)
    for c in range(n_chunks):
        lo, hi = c * FFN_CHUNK, (c + 1) * FFN_CHUNK
        fg, fu = nxt
        if c + 1 < n_chunks:
            nxt = up(c + 1)
        carry = fcarry_ref[:, lo:hi]
        fcarry_ref[:, lo:hi] = fg[TM - SUBLANES:, :]
        fc = (_shift_rows(fg, carry, 2) * fcw_ref[0:1, lo:hi]
              + _shift_rows(fg, carry, 1) * fcw_ref[1:2, lo:hi]
              + fg * fcw_ref[2:3, lo:hi] + fcb_ref[:, lo:hi])
        act = (_gelu(fc) * fu).astype(BF16)
        acc = acc + _dot(act, wfout_ref[lo:hi, :])

    @pl.when(t == nt - 1)
    def _():
        newffn_ref[0] = fcarry_ref[SUBLANES - (FFN_CONV_K - 1):, :]

    y_ref[0] = _ple_tail(acc, p_ref[0], gple_ref[...], wgate_ref, wproj_ref)


def _sample_proj_kernel(x_ref, cos_ref, sin_ref, gmix_ref, win_ref, qg_ref, kg_ref,
                        q_ref, k_ref, v_ref, glu_ref):
    xn = _rms_norm(x_ref[...], gmix_ref[...]).astype(BF16)
    z = _dot(xn, win_ref[...])
    cos = cos_ref[...]
    sin = sin_ref[...]
    for c in range(ATTN_WIDTH // LANES):
        qc = _head_norm_rope(z[:, c * LANES:(c + 1) * LANES], qg_ref[...], cos, sin)
        q_ref[:, c * LANES:(c + 1) * LANES] = qc * ATTN_SCALE
    k_ref[...] = _head_norm_rope(z[:, Q_END:K_END], kg_ref[...], cos, sin)
    v_ref[...] = z[:, K_END:V_END]
    glu_ref[...] = z[:, V_END:V_END + CONV_CH] * _sigmoid(z[:, V_END + CONV_CH:])


def _sample_mixer_kernel(q_ref, kn_ref, vn_ref, kc_ref, vc_ref, glu_ref, st_ref, sink_ref,
                         convw_ref, convb_ref, lng_ref, lnb_ref,
                         o_ref, co_ref):
    bc = SAMPLE_CHUNK
    q = q_ref[...]
    lane = lax.broadcasted_iota(jnp.int32, (bc, LANES), 1)
    lo = lane < HEAD_DIM
    pieces = []
    for hd in range(N_HEADS):
        c, kv, half = hd // 2, hd // GROUP, hd % 2
        blk = q[:, c * LANES:(c + 1) * LANES]
        if half != kv:
            blk = pltpu.roll(blk, HEAD_DIM, axis=1)
        keep = lo if kv == 0 else jnp.logical_not(lo)
        pieces.append(jnp.where(keep, blk, 0.0))
    q8 = jnp.concatenate(pieces, axis=0)
    rows = N_HEADS * bc
    kf = kc_ref[...].reshape(bc * WINDOW, KV_WIDTH)
    vf = vc_ref[...].reshape(bc * WINDOW, KV_WIDTH)
    s = _dot_nt(q8.astype(BF16), kf.astype(BF16))
    rb = lax.broadcasted_iota(jnp.int32, s.shape, 0) & (bc - 1)
    cb = lax.broadcasted_iota(jnp.int32, s.shape, 1) // WINDOW
    s = jnp.where(rb == cb, s, NEG_INF)
    kn8 = jnp.concatenate([kn_ref[...]] * N_HEADS, axis=0)
    vn8 = jnp.concatenate([vn_ref[...]] * N_HEADS, axis=0)
    s_new = jnp.sum(q8 * kn8, axis=-1, keepdims=True)
    sink = sink_ref[...]
    m = jnp.maximum(jnp.maximum(jnp.max(s, axis=-1, keepdims=True), s_new), sink)
    e = jnp.exp(s - m)
    e_new = jnp.exp(s_new - m)
    denom = jnp.sum(e, axis=-1, keepdims=True) + e_new + jnp.exp(sink - m)
    o = (_dot(e.astype(BF16), vf.astype(BF16)) + e_new * vn8) / denom
    o_ref[...] = o.reshape(N_HEADS, bc, KV_WIDTH)

    glu = glu_ref[...]
    acc = glu * convw_ref[CONV_K - 1:CONV_K, :] + convb_ref[...]
    for k in range(CONV_K - 1):
        acc = acc + st_ref[:, k, :] * convw_ref[k:k + 1, :]
    co_ref[...] = _layer_norm_silu(acc, lng_ref[...], lnb_ref[...])


def _sample_tail_kernel(x_ref, attn_ref, co_ref, w8_ref, woutc_ref, st0_ref, st1_ref, p_ref,
                        gffn_ref, wfin_ref, fcw_ref, fcb_ref, wfout_ref,
                        gple_ref, wgate_ref, wproj_ref,
                        y_ref, fg_ref):
    h = x_ref[...] + _dot(co_ref[...].astype(BF16), woutc_ref[...])
    for hd in range(N_HEADS):
        h = h + _dot(attn_ref[hd].astype(BF16), w8_ref[hd])
    y_ref[...] = h
    h = y_ref[...]
    hn = _rms_norm(h, gffn_ref[...]).astype(BF16)
    acc = h
    for c in range(D_FF // FFN_CHUNK):
        lo, hi = c * FFN_CHUNK, (c + 1) * FFN_CHUNK
        fg = _dot(hn, wfin_ref[:, lo:hi])
        fu = _dot(hn, wfin_ref[:, D_FF + lo:D_FF + hi])
        fg_ref[:, lo:hi] = fg
        fc = (st0_ref[:, lo:hi] * fcw_ref[0:1, lo:hi] + st1_ref[:, lo:hi] * fcw_ref[1:2, lo:hi]
              + fg * fcw_ref[2:3, lo:hi] + fcb_ref[:, lo:hi])
        act = (_gelu(fc) * fu).astype(BF16)
        acc = acc + _dot(act, wfout_ref[lo:hi, :])
    y_ref[...] = _ple_tail(acc, p_ref[...], gple_ref[...], wgate_ref, wproj_ref)


def _rope_tables(pos):
    half = HEAD_DIM // 2
    inv_freq = ROPE_THETA ** (-jnp.arange(half, dtype=F32) / half)
    ang = pos.astype(F32)[:, None] * jnp.tile(inv_freq, LANES // half)[None, :]
    sign = jnp.tile(jnp.concatenate([-jnp.ones((half,), F32), jnp.ones((half,), F32)]), LANES // HEAD_DIM)
    return jnp.cos(ang), jnp.sin(ang) * sign[None, :]


def _const_spec(shape):
    zeros = (0,) * len(shape)
    return pl.BlockSpec(shape, lambda *_: zeros, pipeline_mode=pl.Buffered(1))


def _smem_spec():
    return pl.BlockSpec(memory_space=pltpu.SMEM)


def kernel(x_prompt, x_sample, state_attn_k, state_attn_v, state_conv, state_ffn_conv, p_prompt, p_sample, norm_mix_g, w_in, q_norm_g, k_norm_g, attn_sinks, conv_w, conv_b, conv_ln_g, conv_ln_b, w_out, norm_ffn_g, w_ffn_in, ffn_conv_w, ffn_conv_b, w_ffn_out, norm_ple_g, w_ple_gate, w_ple_proj):
    bp, sp, _ = x_prompt.shape
    bs = x_sample.shape[0]
    assert sp % TM == 0 and TM % BLOCK == 0 and bs % SAMPLE_CHUNK == 0
    assert norm_mix_g.shape[0] == 1 and x_sample.shape[1] == 1
    nt = sp // TM

    win = w_in[0].astype(BF16)
    wout = w_out[0].astype(BF16)
    wfin = w_ffn_in[0].astype(BF16)
    wfout = w_ffn_out[0].astype(BF16)
    wgate = w_ple_gate[0].astype(BF16)
    wproj = w_ple_proj[0].astype(BF16)
    gmix = norm_mix_g[0][None, :]
    gffn = norm_ffn_g[0][None, :]
    gple = norm_ple_g[0][None, :]
    qg = jnp.tile(q_norm_g[0], LANES // HEAD_DIM)[None, :]
    kg = jnp.tile(k_norm_g[0], LANES // HEAD_DIM)[None, :]
    sinks = attn_sinks[0]
    convw = conv_w[0]
    convb = conv_b[0][None, :]
    lng = conv_ln_g[0][None, :]
    lnb = conv_ln_b[0][None, :]
    fcw = ffn_conv_w[0]
    fcb = ffn_conv_b[0][None, :]

    cos_p, sin_p = _rope_tables(jnp.arange(sp, dtype=jnp.int32))
    cos_s, sin_s = _rope_tables(PAST_LEN + jnp.arange(1, dtype=jnp.int32))

    tile_spec = lambda w: pl.BlockSpec((1, TM, w), lambda b, t: (b, t, 0))
    state_spec = lambda r, w: pl.BlockSpec((1, r, w), lambda b, t: (b, 0, 0))
    params = pltpu.CompilerParams(dimension_semantics=("arbitrary", "arbitrary"),
                                  vmem_limit_bytes=VMEM_LIMIT_BYTES)
    h_p, newk_p, newv_p, newconv_p = pl.pallas_call(
        _prompt_mixer_kernel,
        grid=(bp, nt),
        in_specs=[tile_spec(D_MODEL),
                  pl.BlockSpec((TM, LANES), lambda b, t: (t, 0)),
                  pl.BlockSpec((TM, LANES), lambda b, t: (t, 0)),
                  _const_spec((1, D_MODEL)), _const_spec((D_MODEL, IN_COLS)),
                  _const_spec((1, LANES)), _const_spec((1, LANES)), _smem_spec(),
                  _const_spec((CONV_K, CONV_CH)), _const_spec((1, CONV_CH)),
                  _const_spec((1, CONV_CH)), _const_spec((1, CONV_CH)),
                  _const_spec((D_MODEL, D_MODEL))],
        out_specs=[tile_spec(D_MODEL), state_spec(WINDOW, KV_WIDTH), state_spec(WINDOW, KV_WIDTH),
                   state_spec(CONV_K - 1, CONV_CH)],
        out_shape=[jax.ShapeDtypeStruct((bp, sp, D_MODEL), F32),
                   jax.ShapeDtypeStruct((bp, WINDOW, KV_WIDTH), F32),
                   jax.ShapeDtypeStruct((bp, WINDOW, KV_WIDTH), F32),
                   jax.ShapeDtypeStruct((bp, CONV_K - 1, CONV_CH), F32)],
        scratch_shapes=[pltpu.VMEM((BLOCK, KV_WIDTH), F32), pltpu.VMEM((BLOCK, KV_WIDTH), F32),
                        pltpu.VMEM((HIST_ROWS + TM, CONV_CH), F32),
                        pltpu.VMEM((SUBLANES - 1, HIST_ROWS + TM, CONV_CH), F32)],
        compiler_params=params,
        name="prompt_mixer",
    )(x_prompt, cos_p, sin_p, gmix, win, qg, kg, sinks, convw, convb, lng, lnb, wout)

    y_p, newffn_p = pl.pallas_call(
        _prompt_ffn_kernel,
        grid=(bp, nt),
        in_specs=[tile_spec(D_MODEL), tile_spec(PLE_DIM),
                  _const_spec((1, D_MODEL)), _const_spec((D_MODEL, 2 * D_FF)),
                  _const_spec((FFN_CONV_K, D_FF)), _const_spec((1, D_FF)),
                  _const_spec((D_FF, D_MODEL)),
                  _const_spec((1, D_MODEL)), _const_spec((D_MODEL, D_MODEL)),
                  _const_spec((PLE_DIM, D_MODEL))],
        out_specs=[tile_spec(D_MODEL), state_spec(FFN_CONV_K - 1, D_FF)],
        out_shape=[jax.ShapeDtypeStruct((bp, sp, D_MODEL), F32),
                   jax.ShapeDtypeStruct((bp, FFN_CONV_K - 1, D_FF), F32)],
        scratch_shapes=[pltpu.VMEM((SUBLANES, D_FF), F32)],
        compiler_params=params,
        name="prompt_ffn",
    )(h_p, p_prompt[0], gffn, wfin, fcw, fcb, wfout, gple, wgate, wproj)

    xs = x_sample[:, 0, :]
    q_s, k_s, v_s, glu_s = pl.pallas_call(
        _sample_proj_kernel,
        out_shape=[jax.ShapeDtypeStruct((bs, ATTN_WIDTH), F32),
                   jax.ShapeDtypeStruct((bs, KV_WIDTH), F32),
                   jax.ShapeDtypeStruct((bs, KV_WIDTH), F32),
                   jax.ShapeDtypeStruct((bs, CONV_CH), F32)],
        compiler_params=pltpu.CompilerParams(vmem_limit_bytes=VMEM_LIMIT_BYTES),
        name="sample_proj",
    )(xs, cos_s, sin_s, gmix, win, qg, kg)

    kc = state_attn_k[0].reshape(bs, WINDOW, KV_WIDTH)
    vc = state_attn_v[0].reshape(bs, WINDOW, KV_WIDTH)
    stc = state_conv[0]
    sink_col = jnp.repeat(sinks, SAMPLE_CHUNK)[:, None]
    bc = SAMPLE_CHUNK
    row_spec = lambda w: pl.BlockSpec((bc, w), lambda i: (i, 0))
    cache_spec = pl.BlockSpec((bc, WINDOW, KV_WIDTH), lambda i: (i, 0, 0))
    attn_s, convo_s = pl.pallas_call(
        _sample_mixer_kernel,
        grid=(bs // bc,),
        in_specs=[row_spec(ATTN_WIDTH), row_spec(KV_WIDTH), row_spec(KV_WIDTH), cache_spec, cache_spec,
                  row_spec(CONV_CH), pl.BlockSpec((bc, CONV_K - 1, CONV_CH), lambda i: (i, 0, 0)),
                  pl.BlockSpec((N_HEADS * bc, 1), lambda i: (0, 0)),
                  pl.BlockSpec((CONV_K, CONV_CH), lambda i: (0, 0)),
                  pl.BlockSpec((1, CONV_CH), lambda i: (0, 0)),
                  pl.BlockSpec((1, CONV_CH), lambda i: (0, 0)),
                  pl.BlockSpec((1, CONV_CH), lambda i: (0, 0))],
        out_specs=[pl.BlockSpec((N_HEADS, bc, KV_WIDTH), lambda i: (0, i, 0)), row_spec(CONV_CH)],
        out_shape=[jax.ShapeDtypeStruct((N_HEADS, bs, KV_WIDTH), F32),
                   jax.ShapeDtypeStruct((bs, CONV_CH), F32)],
        compiler_params=pltpu.CompilerParams(dimension_semantics=("arbitrary",),
                                             vmem_limit_bytes=VMEM_LIMIT_BYTES),
        name="sample_mixer",
    )(q_s, k_s, v_s, kc, vc, glu_s, stc, sink_col, convw, convb, lng, lnb)

    wo_heads = wout[:ATTN_WIDTH].reshape(N_HEADS, HEAD_DIM, D_MODEL)
    zeros_h = jnp.zeros_like(wo_heads)
    head_kv = (jnp.arange(N_HEADS) // GROUP)[:, None, None]
    w8 = jnp.concatenate([jnp.where(head_kv == 0, wo_heads, zeros_h),
                          jnp.where(head_kv == 1, wo_heads, zeros_h)], axis=1)
    y_s, fg_s = pl.pallas_call(
        _sample_tail_kernel,
        out_shape=[jax.ShapeDtypeStruct((bs, D_MODEL), F32), jax.ShapeDtypeStruct((bs, D_FF), F32)],
        compiler_params=pltpu.CompilerParams(vmem_limit_bytes=VMEM_LIMIT_BYTES),
        name="sample_tail",
    )(xs, attn_s, convo_s, w8, wout[ATTN_WIDTH:], state_ffn_conv[0, :, 0, :], state_ffn_conv[0, :, 1, :],
      p_sample[0, :, 0, :], gffn, wfin, fcw, fcb, wfout, gple, wgate, wproj)

    kv_shape = (1, bp, WINDOW, N_KV_HEADS, HEAD_DIM)
    new_k_s = jnp.concatenate([state_attn_k[0][:, 1:], k_s.reshape(bs, 1, N_KV_HEADS, HEAD_DIM)], axis=1)
    new_v_s = jnp.concatenate([state_attn_v[0][:, 1:], v_s.reshape(bs, 1, N_KV_HEADS, HEAD_DIM)], axis=1)
    new_conv_s = jnp.concatenate([state_conv[0][:, 1:], glu_s[:, None, :]], axis=1)
    new_ffn_s = jnp.stack([state_ffn_conv[0, :, 1, :], fg_s], axis=1)
    return (y_p, y_s[:, None, :],
            newk_p.reshape(kv_shape), newv_p.reshape(kv_shape), newconv_p[None], newffn_p[None],
            new_k_s[None], new_v_s[None], new_conv_s[None], new_ffn_s[None])
```

```python
import numpy as np
import jax
import jax.numpy as jnp
from jax import lax
from jax.experimental import pallas as pl
from jax.experimental.pallas import tpu as pltpu

D_MODEL = 1024
N_HEADS = 8
N_KV_HEADS = 2
HEAD_DIM = 64
GROUP = N_HEADS // N_KV_HEADS
ATTN_WIDTH = N_HEADS * HEAD_DIM
KV_WIDTH = N_KV_HEADS * HEAD_DIM
WINDOW = 128
BLOCK = 128
ROPE_THETA = 10000.0
ATTN_SCALE = HEAD_DIM ** -0.5
CONV_CH = D_MODEL - ATTN_WIDTH
CONV_K = 31
Q_END = ATTN_WIDTH
K_END = Q_END + KV_WIDTH
V_END = K_END + KV_WIDTH
IN_COLS = V_END + 2 * CONV_CH
D_FF = 2816
FFN_CONV_K = 3
PLE_DIM = 256
PAST_LEN = 8192
EPS = 1e-6
NEG_INF = -1e30

LANES = 128
SUBLANES = 8
VMEM_LIMIT_BYTES = 56 * 1024 * 1024

TM = 512
HIST_ROWS = 32
FFN_CHUNK = 256
CONV_ROWS = 32
SAMPLE_CHUNK = 16

BF16 = jnp.bfloat16
F32 = jnp.float32


def _dot(a, b):
    return jnp.dot(a, b, preferred_element_type=F32)


def _dot_nt(a, b):
    return lax.dot_general(a, b, (((1,), (1,)), ((), ())), preferred_element_type=F32)


def _rms_norm(x, g):
    return x * lax.rsqrt(jnp.mean(x * x, axis=-1, keepdims=True) + EPS) * g


def _sigmoid(x):
    return 1.0 / (1.0 + jnp.exp(-x))


def _gelu(x):
    return 0.5 * x * (1.0 + lax.erf(x * np.float32(np.sqrt(0.5))))


def _head_norm_rope(x, g, cos, sin):
    lane = lax.broadcasted_iota(jnp.int32, x.shape, 1)
    first_head = lane < HEAD_DIM
    sq = x * x
    ms_a = jnp.sum(jnp.where(first_head, sq, 0.0), axis=-1, keepdims=True)
    ms_b = jnp.sum(jnp.where(first_head, 0.0, sq), axis=-1, keepdims=True)
    ms = jnp.where(first_head, ms_a, ms_b) * (1.0 / HEAD_DIM)
    xn = x * lax.rsqrt(ms + EPS) * g
    first_half = (lane & (HEAD_DIM - 1)) < (HEAD_DIM // 2)
    partner = jnp.where(first_half,
                        pltpu.roll(xn, LANES - HEAD_DIM // 2, axis=1),
                        pltpu.roll(xn, HEAD_DIM // 2, axis=1))
    return xn * cos + partner * sin


def _shift_rows(x, carry, s):
    rolled = pltpu.roll(x, s, axis=0)
    row = lax.broadcasted_iota(jnp.int32, carry.shape, 0)
    head = jnp.where(row < s, pltpu.roll(carry, s, axis=0), rolled[0:SUBLANES])
    return jnp.concatenate([head, rolled[SUBLANES:]], axis=0)


def _layer_norm_silu(x, g, b):
    mu = jnp.mean(x, axis=-1, keepdims=True)
    xc = x - mu
    var = jnp.mean(xc * xc, axis=-1, keepdims=True)
    y = xc * lax.rsqrt(var + EPS) * g + b
    return y * _sigmoid(y)


def _kv_variants(x):
    lane = lax.broadcasted_iota(jnp.int32, x.shape, 1)
    lo = lane < HEAD_DIM
    sw = pltpu.roll(x, HEAD_DIM, axis=1)
    zero = jnp.zeros_like(x)
    return ((jnp.where(lo, x, zero).astype(BF16), jnp.where(lo, zero, sw).astype(BF16)),
            (jnp.where(lo, sw, zero).astype(BF16), jnp.where(lo, zero, x).astype(BF16)))


def _prompt_mixer_kernel(x_ref, cos_ref, sin_ref, gmix_ref, win_ref, qg_ref, kg_ref, sink_ref,
                         convw_ref, convb_ref, lng_ref, lnb_ref, wout_ref,
                         h_ref, newk_ref, newv_ref, newconv_ref,
                         kcarry_ref, vcarry_ref, ext_ref, sh_ref):
    t = pl.program_id(1)
    nt = pl.num_programs(1)

    @pl.when(t == 0)
    def _():
        kcarry_ref[...] = jnp.zeros_like(kcarry_ref)
        vcarry_ref[...] = jnp.zeros_like(vcarry_ref)
        ext_ref[0:HIST_ROWS, :] = jnp.zeros((HIST_ROWS, CONV_CH), F32)

    x = x_ref[0]
    xn = _rms_norm(x, gmix_ref[...]).astype(BF16)
    z = _dot(xn, win_ref[...])

    cos = cos_ref[...]
    sin = sin_ref[...]
    qg = qg_ref[...]
    q_blocks = []
    for c in range(ATTN_WIDTH // LANES):
        qc = _head_norm_rope(z[:, c * LANES:(c + 1) * LANES], qg, cos, sin)
        q_blocks.append((qc * ATTN_SCALE).astype(BF16))
    k_rot = _head_norm_rope(z[:, Q_END:K_END], kg_ref[...], cos, sin)
    v = z[:, K_END:V_END]

    k_ext = jnp.concatenate([kcarry_ref[...], k_rot], axis=0)
    v_ext = jnp.concatenate([vcarry_ref[...], v], axis=0)
    k_var = _kv_variants(k_ext)
    v_var = _kv_variants(v_ext)
    kcarry_ref[...] = k_rot[TM - BLOCK:, :]
    vcarry_ref[...] = v[TM - BLOCK:, :]

    @pl.when(t == nt - 1)
    def _():
        newk_ref[0] = k_rot[TM - WINDOW:, :].T
        newv_ref[0] = v[TM - WINDOW:, :].T

    rows = 2 * BLOCK
    ri = lax.broadcasted_iota(jnp.int32, (rows, 2 * BLOCK), 0) & (BLOCK - 1)
    cj = lax.broadcasted_iota(jnp.int32, (rows, 2 * BLOCK), 1)
    row_id = lax.broadcasted_iota(jnp.int32, (rows, 1), 0)
    prev_off_first = jnp.where(t > 0, 0, BLOCK)
    attn_rows = []
    for n in range(TM // BLOCK):
        prev_off = prev_off_first if n == 0 else 0
        mask = jnp.where(cj < BLOCK, cj - ri - prev_off, ri + BLOCK - cj) >= 0
        r0 = n * BLOCK
        out_cols = [None] * (ATTN_WIDTH // LANES)
        for kv in range(N_KV_HEADS):
            c0, c1 = 2 * kv, 2 * kv + 1
            qs = jnp.concatenate([q_blocks[c0][r0:r0 + BLOCK], q_blocks[c1][r0:r0 + BLOCK]], axis=0)
            for half in range(2):
                kmat = k_var[kv][half][r0:r0 + 2 * BLOCK]
                vmat = v_var[kv][half][r0:r0 + 2 * BLOCK]
                s = jnp.where(mask, _dot_nt(qs, kmat), NEG_INF)
                sink = jnp.where(row_id < BLOCK, sink_ref[2 * c0 + half], sink_ref[2 * c1 + half])
                m = jnp.maximum(jnp.max(s, axis=-1, keepdims=True), sink)
                e = jnp.exp(s - m)
                denom = jnp.sum(e, axis=-1, keepdims=True) + jnp.exp(sink - m)
                o = _dot(e.astype(BF16), vmat) / denom
                out_cols[c0] = o[:BLOCK] if half == 0 else out_cols[c0] + o[:BLOCK]
                out_cols[c1] = o[BLOCK:] if half == 0 else out_cols[c1] + o[BLOCK:]
        attn_rows.append(jnp.concatenate(out_cols, axis=1).astype(BF16))
    attn = jnp.concatenate(attn_rows, axis=0)
    h_attn = x + _dot(attn, wout_ref[0:ATTN_WIDTH, :])

    a = z[:, V_END:V_END + CONV_CH]
    gate = z[:, V_END + CONV_CH:]
    glu = a * _sigmoid(gate)
    ext_ref[HIST_ROWS:, :] = glu
    span = TM + HIST_ROWS - SUBLANES
    for r in range(1, SUBLANES):
        sh_ref[r - 1, 0:span, :] = ext_ref[r:r + span, :]
    base = HIST_ROWS - (CONV_K - 1)
    acc_rows = []
    for r0 in range(0, TM, CONV_ROWS):
        acc = jnp.zeros((CONV_ROWS, CONV_CH), F32) + convb_ref[...]
        for k in range(CONV_K):
            off = base + k
            a8, r = off - off % SUBLANES, off % SUBLANES
            if r == 0:
                tap = ext_ref[r0 + a8:r0 + a8 + CONV_ROWS, :]
            else:
                tap = sh_ref[r - 1, r0 + a8:r0 + a8 + CONV_ROWS, :]
            acc = acc + tap * convw_ref[k:k + 1, :]
        acc_rows.append(acc)
    acc = jnp.concatenate(acc_rows, axis=0)
    conv_o = _layer_norm_silu(acc, lng_ref[...], lnb_ref[...]).astype(BF16)

    @pl.when(t == nt - 1)
    def _():
        newconv_ref[0] = ext_ref[HIST_ROWS + TM - (CONV_K - 1):, :]

    ext_ref[0:HIST_ROWS, :] = ext_ref[TM:TM + HIST_ROWS, :]

    h_ref[0] = h_attn + _dot(conv_o, wout_ref[ATTN_WIDTH:, :])


def _ple_tail(h, p, gple, wgate_ref, wproj_ref):
    gate = _sigmoid(_dot(_rms_norm(h, gple).astype(BF16), wgate_ref[...]))
    return h + _dot(p.astype(BF16), wproj_ref[...]) * gate


def _prompt_ffn_kernel(h_ref, p_ref, gffn_ref, wfin_ref, fcw_ref, fcb_ref, wfout_ref,
                       gple_ref, wgate_ref, wproj_ref,
                       y_ref, newffn_ref,
                       fcarry_ref):
    t = pl.program_id(1)
    nt = pl.num_programs(1)

    @pl.when(t == 0)
    def _():
        fcarry_ref[...] = jnp.zeros_like(fcarry_ref)

    h = h_ref[0]
    hn = _rms_norm(h, gffn_ref[...]).astype(BF16)
    acc = h
    n_chunks = D_FF // FFN_CHUNK
    up = lambda c: (_dot(hn, wfin_ref[:, c * FFN_CHUNK:(c + 1) * FFN_CHUNK]),
                    _dot(hn, wfin_ref[:, D_FF + c * FFN_CHUNK:D_FF + (c + 1) * FFN_CHUNK]))
    nxt = up(0)
    for c in range(n_chunks):
        lo, hi = c * FFN_CHUNK, (c + 1) * FFN_CHUNK
        fg, fu = nxt
        if c + 1 < n_chunks:
            nxt = up(c + 1)
        carry = fcarry_ref[:, lo:hi]
        fcarry_ref[:, lo:hi] = fg[TM - SUBLANES:, :]
        fc = (_shift_rows(fg, carry, 2) * fcw_ref[0:1, lo:hi]
              + _shift_rows(fg, carry, 1) * fcw_ref[1:2, lo:hi]
              + fg * fcw_ref[2:3, lo:hi] + fcb_ref[:, lo:hi])
        act = (_gelu(fc) * fu).astype(BF16)
        acc = acc + _dot(act, wfout_ref[lo:hi, :])

    @pl.when(t == nt - 1)
    def _():
        newffn_ref[0] = fcarry_ref[SUBLANES - (FFN_CONV_K - 1):, :]

    y_ref[0] = _ple_tail(acc, p_ref[0], gple_ref[...], wgate_ref, wproj_ref)


def _sample_proj_kernel(x_ref, cos_ref, sin_ref, gmix_ref, win_ref, qg_ref, kg_ref,
                        q_ref, k_ref, v_ref, kt_ref, vt_ref, glu_ref):
    xn = _rms_norm(x_ref[...], gmix_ref[...]).astype(BF16)
    z = _dot(xn, win_ref[...])
    cos = cos_ref[...]
    sin = sin_ref[...]
    for c in range(ATTN_WIDTH // LANES):
        qc = _head_norm_rope(z[:, c * LANES:(c + 1) * LANES], qg_ref[...], cos, sin)
        q_ref[:, c * LANES:(c + 1) * LANES] = qc * ATTN_SCALE
    k = _head_norm_rope(z[:, Q_END:K_END], kg_ref[...], cos, sin)
    v = z[:, K_END:V_END]
    k_ref[...] = k
    v_ref[...] = v
    kt_ref[...] = k.T
    vt_ref[...] = v.T
    glu_ref[...] = z[:, V_END:V_END + CONV_CH] * _sigmoid(z[:, V_END + CONV_CH:])


def _sample_mixer_kernel(q_ref, kn_ref, vn_ref, knt_ref, vnt_ref, kct_ref, vct_ref, glu_ref, st_ref, sink_ref,
                         convw_ref, convb_ref, lng_ref, lnb_ref,
                         o_ref, co_ref, newk_ref, newv_ref, newst_ref,
                         q8_ref):
    bc = SAMPLE_CHUNK
    chunk = pl.program_id(0)
    q = q_ref[...]
    lane = lax.broadcasted_iota(jnp.int32, (bc, LANES), 1)
    lo = lane < HEAD_DIM
    for hd in range(N_HEADS):
        c, kv, half = hd // 2, hd // GROUP, hd % 2
        blk = q[:, c * LANES:(c + 1) * LANES]
        if half != kv:
            blk = pltpu.roll(blk, HEAD_DIM, axis=1)
        keep = lo if kv == 0 else jnp.logical_not(lo)
        q8_ref[hd * bc:(hd + 1) * bc, :] = jnp.where(keep, blk, 0.0)

    s_rows, snew_rows = [], []
    for b in range(bc):
        qb = q8_ref[pl.ds(b, N_HEADS, stride=bc), :]
        s_rows.append(_dot(qb.astype(BF16), kct_ref[b].astype(BF16)))
        snew_rows.append(jnp.sum(qb * kn_ref[b:b + 1, :], axis=-1, keepdims=True))
    s = jnp.concatenate(s_rows, axis=0)
    s_new = jnp.concatenate(snew_rows, axis=0)
    sink = sink_ref[...]
    m = jnp.maximum(jnp.maximum(jnp.max(s, axis=-1, keepdims=True), s_new), sink)
    e = jnp.exp(s - m)
    e_new = jnp.exp(s_new - m)
    denom = jnp.sum(e, axis=-1, keepdims=True) + e_new + jnp.exp(sink - m)
    o_rows = []
    for b in range(bc):
        r0, r1 = b * N_HEADS, (b + 1) * N_HEADS
        pv = _dot_nt(e[r0:r1].astype(BF16), vct_ref[b].astype(BF16))
        o_rows.append(pv + e_new[r0:r1] * vn_ref[b:b + 1, :])
    o_ref[...] = jnp.concatenate(o_rows, axis=0) / denom

    lane_j = lax.broadcasted_iota(jnp.int32, (KV_WIDTH, WINDOW), 1)
    newest = lane_j == WINDOW - 1
    knt = knt_ref[...]
    vnt = vnt_ref[...]
    for b in range(bc):
        col_shift = WINDOW - 1 - (chunk * bc + b)
        newk_ref[b] = jnp.where(newest, pltpu.roll(knt, col_shift, axis=1),
                                pltpu.roll(kct_ref[b], WINDOW - 1, axis=1))
        newv_ref[b] = jnp.where(newest, pltpu.roll(vnt, col_shift, axis=1),
                                pltpu.roll(vct_ref[b], WINDOW - 1, axis=1))

    glu = glu_ref[...]
    acc = glu * convw_ref[CONV_K - 1:CONV_K, :] + convb_ref[...]
    for k in range(CONV_K - 1):
        acc = acc + st_ref[k] * convw_ref[k:k + 1, :]
    co_ref[...] = _layer_norm_silu(acc, lng_ref[...], lnb_ref[...])
    for k in range(CONV_K - 2):
        newst_ref[k] = st_ref[k + 1]
    newst_ref[CONV_K - 2] = glu


def _sample_tail_kernel(x_ref, attn_ref, co_ref, w8_ref, woutc_ref, stf_ref, p_ref,
                        gffn_ref, wfin_ref, fcw_ref, fcb_ref, wfout_ref,
                        gple_ref, wgate_ref, wproj_ref,
                        y_ref, newf_ref):
    bs = x_ref.shape[0]
    h = x_ref[...] + _dot(co_ref[...].astype(BF16), woutc_ref[...])
    for hd in range(N_HEADS):
        h = h + _dot(attn_ref[pl.ds(hd, bs, stride=N_HEADS), :].astype(BF16), w8_ref[hd])
    y_ref[...] = h
    h = y_ref[...]
    hn = _rms_norm(h, gffn_ref[...]).astype(BF16)
    acc = h
    for c in range(D_FF // FFN_CHUNK):
        lo, hi = c * FFN_CHUNK, (c + 1) * FFN_CHUNK
        fg = _dot(hn, wfin_ref[:, lo:hi])
        fu = _dot(hn, wfin_ref[:, D_FF + lo:D_FF + hi])
        st0 = stf_ref[:, 0, lo:hi]
        st1 = stf_ref[:, 1, lo:hi]
        newf_ref[:, 0, lo:hi] = st1
        newf_ref[:, 1, lo:hi] = fg
        fc = st0 * fcw_ref[0:1, lo:hi] + st1 * fcw_ref[1:2, lo:hi] + fg * fcw_ref[2:3, lo:hi] + fcb_ref[:, lo:hi]
        act = (_gelu(fc) * fu).astype(BF16)
        acc = acc + _dot(act, wfout_ref[lo:hi, :])
    y_ref[...] = _ple_tail(acc, p_ref[...], gple_ref[...], wgate_ref, wproj_ref)


def _rope_tables(pos):
    half = HEAD_DIM // 2
    inv_freq = ROPE_THETA ** (-jnp.arange(half, dtype=F32) / half)
    ang = pos.astype(F32)[:, None] * jnp.tile(inv_freq, LANES // half)[None, :]
    sign = jnp.tile(jnp.concatenate([-jnp.ones((half,), F32), jnp.ones((half,), F32)]), LANES // HEAD_DIM)
    return jnp.cos(ang), jnp.sin(ang) * sign[None, :]


def _const_spec(shape):
    zeros = (0,) * len(shape)
    return pl.BlockSpec(shape, lambda *_: zeros, pipeline_mode=pl.Buffered(1))


def _smem_spec():
    return pl.BlockSpec(memory_space=pltpu.SMEM)


def kernel(x_prompt, x_sample, state_attn_k, state_attn_v, state_conv, state_ffn_conv, p_prompt, p_sample, norm_mix_g, w_in, q_norm_g, k_norm_g, attn_sinks, conv_w, conv_b, conv_ln_g, conv_ln_b, w_out, norm_ffn_g, w_ffn_in, ffn_conv_w, ffn_conv_b, w_ffn_out, norm_ple_g, w_ple_gate, w_ple_proj):
    bp, sp, _ = x_prompt.shape
    bs = x_sample.shape[0]
    assert sp % TM == 0 and TM % BLOCK == 0 and bs % SAMPLE_CHUNK == 0
    assert norm_mix_g.shape[0] == 1 and x_sample.shape[1] == 1
    nt = sp // TM

    win = w_in[0].astype(BF16)
    wout = w_out[0].astype(BF16)
    wfin = w_ffn_in[0].astype(BF16)
    wfout = w_ffn_out[0].astype(BF16)
    wgate = w_ple_gate[0].astype(BF16)
    wproj = w_ple_proj[0].astype(BF16)
    gmix = norm_mix_g[0][None, :]
    gffn = norm_ffn_g[0][None, :]
    gple = norm_ple_g[0][None, :]
    qg = jnp.tile(q_norm_g[0], LANES // HEAD_DIM)[None, :]
    kg = jnp.tile(k_norm_g[0], LANES // HEAD_DIM)[None, :]
    sinks = attn_sinks[0]
    convw = conv_w[0]
    convb = conv_b[0][None, :]
    lng = conv_ln_g[0][None, :]
    lnb = conv_ln_b[0][None, :]
    fcw = ffn_conv_w[0]
    fcb = ffn_conv_b[0][None, :]

    cos_p, sin_p = _rope_tables(jnp.arange(sp, dtype=jnp.int32))
    cos_s, sin_s = _rope_tables(PAST_LEN + jnp.arange(1, dtype=jnp.int32))

    tile_spec = lambda w: pl.BlockSpec((1, TM, w), lambda b, t: (b, t, 0))
    state_spec = lambda r, w: pl.BlockSpec((1, r, w), lambda b, t: (b, 0, 0))
    params = pltpu.CompilerParams(dimension_semantics=("arbitrary", "arbitrary"),
                                  vmem_limit_bytes=VMEM_LIMIT_BYTES)
    h_p, newk_p, newv_p, newconv_p = pl.pallas_call(
        _prompt_mixer_kernel,
        grid=(bp, nt),
        in_specs=[tile_spec(D_MODEL),
                  pl.BlockSpec((TM, LANES), lambda b, t: (t, 0)),
                  pl.BlockSpec((TM, LANES), lambda b, t: (t, 0)),
                  _const_spec((1, D_MODEL)), _const_spec((D_MODEL, IN_COLS)),
                  _const_spec((1, LANES)), _const_spec((1, LANES)), _smem_spec(),
                  _const_spec((CONV_K, CONV_CH)), _const_spec((1, CONV_CH)),
                  _const_spec((1, CONV_CH)), _const_spec((1, CONV_CH)),
                  _const_spec((D_MODEL, D_MODEL))],
        out_specs=[tile_spec(D_MODEL), state_spec(KV_WIDTH, WINDOW), state_spec(KV_WIDTH, WINDOW),
                   state_spec(CONV_K - 1, CONV_CH)],
        out_shape=[jax.ShapeDtypeStruct((bp, sp, D_MODEL), F32),
                   jax.ShapeDtypeStruct((bp, KV_WIDTH, WINDOW), F32),
                   jax.ShapeDtypeStruct((bp, KV_WIDTH, WINDOW), F32),
                   jax.ShapeDtypeStruct((bp, CONV_K - 1, CONV_CH), F32)],
        scratch_shapes=[pltpu.VMEM((BLOCK, KV_WIDTH), F32), pltpu.VMEM((BLOCK, KV_WIDTH), F32),
                        pltpu.VMEM((HIST_ROWS + TM, CONV_CH), F32),
                        pltpu.VMEM((SUBLANES - 1, HIST_ROWS + TM, CONV_CH), F32)],
        compiler_params=params,
        name="prompt_mixer",
    )(x_prompt, cos_p, sin_p, gmix, win, qg, kg, sinks, convw, convb, lng, lnb, wout)

    y_p, newffn_p = pl.pallas_call(
        _prompt_ffn_kernel,
        grid=(bp, nt),
        in_specs=[tile_spec(D_MODEL), tile_spec(PLE_DIM),
                  _const_spec((1, D_MODEL)), _const_spec((D_MODEL, 2 * D_FF)),
                  _const_spec((FFN_CONV_K, D_FF)), _const_spec((1, D_FF)),
                  _const_spec((D_FF, D_MODEL)),
                  _const_spec((1, D_MODEL)), _const_spec((D_MODEL, D_MODEL)),
                  _const_spec((PLE_DIM, D_MODEL))],
        out_specs=[tile_spec(D_MODEL), state_spec(FFN_CONV_K - 1, D_FF)],
        out_shape=[jax.ShapeDtypeStruct((bp, sp, D_MODEL), F32),
                   jax.ShapeDtypeStruct((bp, FFN_CONV_K - 1, D_FF), F32)],
        scratch_shapes=[pltpu.VMEM((SUBLANES, D_FF), F32)],
        compiler_params=params,
        name="prompt_ffn",
    )(h_p, p_prompt[0], gffn, wfin, fcw, fcb, wfout, gple, wgate, wproj)

    xs = x_sample[:, 0, :]
    f32 = lambda *shape: jax.ShapeDtypeStruct(shape, F32)
    q_s, k_s, v_s, kt_s, vt_s, glu_s = pl.pallas_call(
        _sample_proj_kernel,
        out_shape=[f32(bs, ATTN_WIDTH), f32(bs, KV_WIDTH), f32(bs, KV_WIDTH),
                   f32(KV_WIDTH, bs), f32(KV_WIDTH, bs), f32(bs, CONV_CH)],
        compiler_params=pltpu.CompilerParams(vmem_limit_bytes=VMEM_LIMIT_BYTES),
        name="sample_proj",
    )(xs, cos_s, sin_s, gmix, win, qg, kg)

    kct = jnp.transpose(state_attn_k[0], (0, 2, 3, 1)).reshape(bs, KV_WIDTH, WINDOW)
    vct = jnp.transpose(state_attn_v[0], (0, 2, 3, 1)).reshape(bs, KV_WIDTH, WINDOW)
    stc = jnp.transpose(state_conv[0], (1, 0, 2))
    stf = state_ffn_conv[0]
    bc = SAMPLE_CHUNK
    sink_col = jnp.tile(sinks, bc)[:, None]
    row_spec = lambda w: pl.BlockSpec((bc, w), lambda i: (i, 0))
    full_spec = lambda *shape: pl.BlockSpec(shape, lambda i: (0,) * len(shape))
    cache_spec = pl.BlockSpec((bc, KV_WIDTH, WINDOW), lambda i: (i, 0, 0))
    hist_spec = pl.BlockSpec((CONV_K - 1, bc, CONV_CH), lambda i: (0, i, 0))
    attn_s, convo_s, newkt_s, newvt_s, newst_s = pl.pallas_call(
        _sample_mixer_kernel,
        grid=(bs // bc,),
        in_specs=[row_spec(ATTN_WIDTH), row_spec(KV_WIDTH), row_spec(KV_WIDTH),
                  full_spec(KV_WIDTH, bs), full_spec(KV_WIDTH, bs), cache_spec, cache_spec,
                  row_spec(CONV_CH), hist_spec, full_spec(N_HEADS * bc, 1),
                  full_spec(CONV_K, CONV_CH), full_spec(1, CONV_CH), full_spec(1, CONV_CH),
                  full_spec(1, CONV_CH)],
        out_specs=[pl.BlockSpec((N_HEADS * bc, KV_WIDTH), lambda i: (i, 0)), row_spec(CONV_CH),
                   cache_spec, cache_spec, hist_spec],
        out_shape=[f32(N_HEADS * bs, KV_WIDTH), f32(bs, CONV_CH),
                   f32(bs, KV_WIDTH, WINDOW), f32(bs, KV_WIDTH, WINDOW), f32(CONV_K - 1, bs, CONV_CH)],
        scratch_shapes=[pltpu.VMEM((N_HEADS * bc, KV_WIDTH), F32)],
        compiler_params=pltpu.CompilerParams(dimension_semantics=("arbitrary",),
                                             vmem_limit_bytes=VMEM_LIMIT_BYTES),
        name="sample_mixer",
    )(q_s, k_s, v_s, kt_s, vt_s, kct, vct, glu_s, stc, sink_col, convw, convb, lng, lnb)

    wo_heads = wout[:ATTN_WIDTH].reshape(N_HEADS, HEAD_DIM, D_MODEL)
    zeros_h = jnp.zeros_like(wo_heads)
    head_kv = (jnp.arange(N_HEADS) // GROUP)[:, None, None]
    w8 = jnp.concatenate([jnp.where(head_kv == 0, wo_heads, zeros_h),
                          jnp.where(head_kv == 1, wo_heads, zeros_h)], axis=1)
    y_s, newf_s = pl.pallas_call(
        _sample_tail_kernel,
        out_shape=[f32(bs, D_MODEL), f32(bs, FFN_CONV_K - 1, D_FF)],
        compiler_params=pltpu.CompilerParams(vmem_limit_bytes=VMEM_LIMIT_BYTES),
        name="sample_tail",
    )(xs, attn_s, convo_s, w8, wout[ATTN_WIDTH:], stf, p_sample[0, :, 0, :],
      gffn, wfin, fcw, fcb, wfout, gple, wgate, wproj)

    to_cache = lambda a: jnp.transpose(a.reshape(a.shape[0], N_KV_HEADS, HEAD_DIM, WINDOW), (0, 3, 1, 2))[None]
    return (y_p, y_s[:, None, :],
            to_cache(newk_p), to_cache(newv_p), newconv_p[None], newffn_p[None],
            to_cache(newkt_s), to_cache(newvt_s), jnp.transpose(newst_s, (1, 0, 2))[None], newf_s[None])
```
